```python
import math
import jax, jax.numpy as jnp
from jax import lax
import numpy as np

D_MODEL = 2048
BATCH = 4
SEQ = 2048
DEPTH = 4
DEC_BATCH = 8
DEC_SEQ = 4
PAST_LEN = 16384
PAGE_SIZE = 128

N_MIXERS = 3
N_FOX = (DEPTH + 2) // 3
N_MLSTM = (DEPTH + 1) // 3
N_GDN = DEPTH // 3
EPS = 1e-6
FOX_HEADS = 16
FOX_HEAD_DIM = D_MODEL // FOX_HEADS
FOX_BLOCK = 128
FOX_FGATE_BIAS = 3.0
MLSTM_HEADS = 4
MLSTM_DV = D_MODEL // MLSTM_HEADS
MLSTM_DQK = MLSTM_DV // 2
MLSTM_QK = MLSTM_HEADS * MLSTM_DQK
MLSTM_CHUNK = 64
GATE_SOFTCAP = 15.0
MLSTM_IGATE_BIAS = -2.0
MLSTM_FGATE_BIAS = 3.0
GDN_K_HEADS = 16
GDN_V_HEADS = 32
GDN_DK = D_MODEL // GDN_K_HEADS
GDN_DV = 128
GDN_QK = GDN_K_HEADS * GDN_DK
GDN_V = GDN_V_HEADS * GDN_DV
GDN_CONV = 4
GDN_CONV_DIM = 2 * GDN_QK + GDN_V
GDN_CHUNK = 64
D_FF = ((8 * D_MODEL // 3 + 127) // 128) * 128

kernel_name = 'fox_mlstm_gdn_macaron_decode_step'

F32 = jnp.float32


def rmsnorm(x, g):
    xf = x.astype(F32)
    y = xf * lax.rsqrt(jnp.mean(xf * xf, axis=-1, keepdims=True) + EPS)
    return (y * g.astype(F32)).astype(x.dtype)


def l2norm(x):
    xf = x.astype(F32)
    return xf * lax.rsqrt(jnp.sum(xf * xf, axis=-1, keepdims=True) + EPS)


def half_ffn(x, g_pre, g_post, w_in, w_down):
    h = rmsnorm(x, g_pre) @ w_in
    gate, up = jnp.split(h, 2, axis=-1)
    y = (jax.nn.silu(gate) * up) @ w_down
    return x + 0.5 * rmsnorm(y, g_post)


def fox_project(h, w_in, b_f):
    B, T, _ = h.shape
    proj = h @ w_in
    q = proj[..., :D_MODEL].reshape(B, T, FOX_HEADS, FOX_HEAD_DIM)
    k = proj[..., D_MODEL:2 * D_MODEL].reshape(B, T, FOX_HEADS, FOX_HEAD_DIM)
    v = proj[..., 2 * D_MODEL:3 * D_MODEL].reshape(B, T, FOX_HEADS, FOX_HEAD_DIM)
    logf = jax.nn.log_sigmoid((proj[..., 3 * D_MODEL:] + b_f).astype(F32))
    return q, k, v, logf


def fox_attend_prompt(q, k, v, logf):
    B, T, H, Dh = q.shape
    nb = T // FOX_BLOCK
    scale = Dh ** -0.5
    c = jnp.cumsum(logf, axis=1).transpose(0, 2, 1)
    kpos = jnp.arange(T)

    def block(args):
        qb, cb, start = args
        s = jnp.einsum('bqhd,bkhd->bhqk', qb, k, preferred_element_type=F32) * scale
        s = s + cb[..., :, None] - c[..., None, :]
        qpos = start + jnp.arange(FOX_BLOCK)
        s = jnp.where(kpos[None, :] <= qpos[:, None], s, -jnp.inf)
        p = jax.nn.softmax(s, axis=-1).astype(v.dtype)
        return jnp.einsum('bhqk,bkhd->bqhd', p, v)

    qb = q.reshape(B, nb, FOX_BLOCK, H, Dh).swapaxes(0, 1)
    cb = c.reshape(B, H, nb, FOX_BLOCK).transpose(2, 0, 1, 3)
    starts = jnp.arange(nb) * FOX_BLOCK
    out = lax.map(block, (qb, cb, starts))
    return out.swapaxes(0, 1).reshape(B, T, H * Dh)


def fox_attend_sample(q, k, v, logf, k_past, v_past, logf_past):
    B, T, H, Dh = q.shape
    P = k_past.shape[1]
    scale = Dh ** -0.5
    lp = logf_past.astype(F32)
    c_past = (lp - lax.cumsum(lp, axis=1, reverse=True)).transpose(0, 2, 1)
    c_new = jnp.cumsum(logf, axis=1).transpose(0, 2, 1)
    s_past = jnp.einsum('bqhd,bkhd->bhqk', q, k_past, preferred_element_type=F32) * scale
    s_past = s_past + c_new[..., :, None] - c_past[..., None, :]
    s_new = jnp.einsum('bqhd,bkhd->bhqk', q, k, preferred_element_type=F32) * scale
    s_new = s_new + c_new[..., :, None] - c_new[..., None, :]
    s_new = jnp.where(jnp.tril(jnp.ones((T, T), bool)), s_new, -jnp.inf)
    p = jax.nn.softmax(jnp.concatenate([s_past, s_new], axis=-1), axis=-1).astype(v.dtype)
    out = (jnp.einsum('bhqk,bkhd->bqhd', p[..., :P], v_past)
           + jnp.einsum('bhqk,bkhd->bqhd', p[..., P:], v))
    return out.reshape(B, T, H * Dh)


def mlstm_chunkwise(q, k, v, ig, logf, C0, n0, m0, chunk):
    B, T, H, _ = q.shape
    nc = T // chunk
    scale = MLSTM_DQK ** -0.5
    causal = jnp.tril(jnp.ones((chunk, chunk), bool))

    def to_chunks(a):
        return a.reshape(B, nc, chunk, *a.shape[2:]).swapaxes(0, 1)

    def step(carry, xs):
        C, n, m = carry
        qc, kc, vc, ic, fc = xs
        qc = qc.astype(F32) * scale
        kc = kc.astype(F32)
        vc = vc.astype(F32)
        b = jnp.cumsum(fc, axis=1).transpose(0, 2, 1)
        it = ic.transpose(0, 2, 1)
        log_d = jnp.where(causal, b[..., :, None] - b[..., None, :] + it[..., None, :], -jnp.inf)
        log_inter = b + m[..., None]
        m_t = jnp.maximum(log_inter, jnp.max(log_d, axis=-1))
        d = jnp.exp(log_d - m_t[..., None])
        a = jnp.exp(log_inter - m_t)
        s = jnp.einsum('blhd,bshd->bhls', qc, kc) * d
        num = jnp.einsum('bhls,bshv->bhlv', s, vc) + a[..., None] * jnp.einsum('blhd,bhdv->bhlv', qc, C)
        den = jnp.sum(s, axis=-1) + a * jnp.einsum('blhd,bhd->bhl', qc, n)
        h = num / jnp.maximum(jnp.abs(den), jnp.exp(-m_t))[..., None]
        b_last = b[..., -1]
        log_w = b_last[..., None] - b + it
        m_new = jnp.maximum(b_last + m, jnp.max(log_w, axis=-1))
        w = jnp.exp(log_w - m_new[..., None]).transpose(0, 2, 1)
        decay = jnp.exp(b_last + m - m_new)
        kw = kc * w[..., None]
        C_new = decay[..., None, None] * C + jnp.einsum('bshd,bshv->bhdv', kw, vc)
        n_new = decay[..., None] * n + jnp.sum(kw, axis=1)
        return (C_new, n_new, m_new), h.transpose(0, 2, 1, 3)

    xs = (to_chunks(q), to_chunks(k), to_chunks(v), to_chunks(ig), to_chunks(logf))
    (C, n, m), hs = lax.scan(step, (C0, n0, m0), xs)
    return hs.swapaxes(0, 1).reshape(B, T, H, MLSTM_DV), C, n, m


def mlstm_mixer(h, w_in, b_gates, norm_g, w_out, C0, n0, m0, chunk):
    B, T, _ = h.shape
    H = MLSTM_HEADS
    proj = h @ w_in
    q = proj[..., :MLSTM_QK].reshape(B, T, H, MLSTM_DQK)
    k = proj[..., MLSTM_QK:2 * MLSTM_QK].reshape(B, T, H, MLSTM_DQK)
    v = proj[..., 2 * MLSTM_QK:2 * MLSTM_QK + D_MODEL].reshape(B, T, H, MLSTM_DV)
    o = proj[..., 2 * MLSTM_QK + D_MODEL:2 * MLSTM_QK + 2 * D_MODEL]
    gates = (proj[..., 2 * MLSTM_QK + 2 * D_MODEL:] + b_gates).astype(F32)
    gates = GATE_SOFTCAP * jnp.tanh(gates / GATE_SOFTCAP)
    ig = gates[..., :H]
    logf = jax.nn.log_sigmoid(gates[..., H:])
    ht, C, n, m = mlstm_chunkwise(q, k, v, ig, logf, C0.astype(F32), n0.astype(F32), m0.astype(F32), chunk)
    hn = rmsnorm(ht, norm_g.reshape(H, MLSTM_DV)).reshape(B, T, D_MODEL)
    out = (jax.nn.sigmoid(o.astype(F32)) * hn).astype(h.dtype) @ w_out
    return out, C, n, m


def gated_delta_chunked(q, k, v, g, beta, S0, chunk):
    B, T, H, dk = q.shape
    nc = T // chunk
    scale = dk ** -0.5
    causal = jnp.tril(jnp.ones((chunk, chunk), bool))
    strict = jnp.tril(jnp.ones((chunk, chunk), bool), -1)
    eye = jnp.eye(chunk, dtype=F32)

    def to_chunks(a):
        return a.reshape(B, nc, chunk, *a.shape[2:]).swapaxes(0, 1)

    def step(S, xs):
        qc, kc, vc, gc, bc = xs
        qc = qc.astype(F32).transpose(0, 2, 1, 3) * scale
        kc = kc.astype(F32).transpose(0, 2, 1, 3)
        vc = vc.astype(F32).transpose(0, 2, 1, 3)
        gcum = jnp.cumsum(gc, axis=1).transpose(0, 2, 1)
        bt = bc.transpose(0, 2, 1)
        decay = jnp.exp(jnp.where(causal, gcum[..., :, None] - gcum[..., None, :], -jnp.inf))
        kk = jnp.einsum('bhld,bhsd->bhls', kc, kc)
        A = jnp.where(strict, bt[..., :, None] * kk * decay, 0.0)
        Tm = lax.linalg.triangular_solve(A + eye, jnp.broadcast_to(eye, A.shape),
                                         left_side=True, lower=True, unit_diagonal=True)
        u = Tm @ (vc * bt[..., None])
        w = Tm @ (kc * (bt * jnp.exp(gcum))[..., None])
        v_new = u - w @ S
        attn = jnp.where(causal, jnp.einsum('bhld,bhsd->bhls', qc, kc) * decay, 0.0)
        out = (qc * jnp.exp(gcum)[..., None]) @ S + attn @ v_new
        g_last = gcum[..., -1]
        S_new = (S * jnp.exp(g_last)[..., None, None]
                 + jnp.einsum('bhld,bhlv->bhdv', kc * jnp.exp(g_last[..., None] - gcum)[..., None], v_new))
        return S_new, out

    xs = (to_chunks(q), to_chunks(k), to_chunks(v), to_chunks(g), to_chunks(beta))
    S, outs = lax.scan(step, S0, xs)
    return outs.transpose(1, 0, 3, 2, 4).reshape(B, T, H, v.shape[-1]), S


def gdn_mixer(h, w_in, conv_w, A_log, dt_bias, norm_g, w_out, S0, conv_buf, chunk):
    B, T, _ = h.shape
    proj = h @ w_in
    mixed = proj[..., :GDN_CONV_DIM]
    z = proj[..., GDN_CONV_DIM:GDN_CONV_DIM + GDN_V]
    b_pre = proj[..., GDN_CONV_DIM + GDN_V:GDN_CONV_DIM + GDN_V + GDN_V_HEADS]
    a_pre = proj[..., GDN_CONV_DIM + GDN_V + GDN_V_HEADS:]
    xp = jnp.concatenate([conv_buf.astype(mixed.dtype), mixed], axis=1)
    conv = xp[:, 0:T] * conv_w[0]
    for j in range(1, GDN_CONV):
        conv = conv + xp[:, j:j + T] * conv_w[j]
    conv = jax.nn.silu(conv)
    new_buf = xp[:, T:]
    rep = GDN_V_HEADS // GDN_K_HEADS
    q = jnp.repeat(l2norm(conv[..., :GDN_QK].reshape(B, T, GDN_K_HEADS, GDN_DK)), rep, axis=2)
    k = jnp.repeat(l2norm(conv[..., GDN_QK:2 * GDN_QK].reshape(B, T, GDN_K_HEADS, GDN_DK)), rep, axis=2)
    v = conv[..., 2 * GDN_QK:].reshape(B, T, GDN_V_HEADS, GDN_DV)
    beta = jax.nn.sigmoid(b_pre.astype(F32))
    g = -jnp.exp(A_log.astype(F32)) * jax.nn.softplus(a_pre.astype(F32) + dt_bias.astype(F32))
    core, S = gated_delta_chunked(q, k, v, g, beta, S0.astype(F32), chunk)
    core = rmsnorm(core, norm_g) * jax.nn.silu(z.reshape(B, T, GDN_V_HEADS, GDN_DV).astype(F32))
    out = core.reshape(B, T, GDN_V).astype(h.dtype) @ w_out
    return out, S, new_buf


def setup_inputs(seed: int = 0) -> dict:
    key = jax.random.key(seed)
    ks = jax.random.split(key, 32)
    n_pages = PAST_LEN // PAGE_SIZE
    n_pool = (DEC_BATCH * n_pages * 5) // 4

    def nrm(k, shape, scale=1.0):
        return jax.random.normal(k, shape, F32) * scale

    x_prompt = nrm(ks[0], (BATCH, SEQ, D_MODEL))
    x_sample = nrm(ks[1], (DEC_BATCH, DEC_SEQ, D_MODEL))
    cache_k = nrm(ks[2], (N_FOX, n_pool, PAGE_SIZE, FOX_HEADS, FOX_HEAD_DIM))
    cache_v = nrm(ks[3], (N_FOX, n_pool, PAGE_SIZE, FOX_HEADS, FOX_HEAD_DIM))
    cache_logf = jax.nn.log_sigmoid(FOX_FGATE_BIAS + nrm(ks[4], (N_FOX, n_pool, PAGE_SIZE, FOX_HEADS)))
    page_table = jax.random.permutation(ks[5], n_pool)[:DEC_BATCH * n_pages].reshape(DEC_BATCH, n_pages).astype(jnp.int32)
    state_mlstm_C = nrm(ks[6], (N_MLSTM, DEC_BATCH, MLSTM_HEADS, MLSTM_DQK, MLSTM_DV), 0.5)
    state_mlstm_n = nrm(ks[7], (N_MLSTM, DEC_BATCH, MLSTM_HEADS, MLSTM_DQK), 0.5)
    state_mlstm_m = nrm(ks[8], (N_MLSTM, DEC_BATCH, MLSTM_HEADS), 0.5)
    state_gdn_S = nrm(ks[9], (N_GDN, DEC_BATCH, GDN_V_HEADS, GDN_DK, GDN_DV), 0.1)
    state_gdn_conv = nrm(ks[10], (N_GDN, DEC_BATCH, GDN_CONV - 1, GDN_CONV_DIM))

    norm_gains = 1.0 + nrm(ks[11], (DEPTH, 6, D_MODEL), 0.02)
    ffn_w_in = nrm(ks[12], (DEPTH, 2, D_MODEL, 2 * D_FF), D_MODEL ** -0.5)
    ffn_w_down = nrm(ks[13], (DEPTH, 2, D_FF, D_MODEL), D_FF ** -0.5)

    fox_w_in = nrm(ks[14], (N_FOX, D_MODEL, 3 * D_MODEL + FOX_HEADS), D_MODEL ** -0.5)
    fox_b_f = FOX_FGATE_BIAS + nrm(ks[15], (N_FOX, FOX_HEADS), 0.1)
    fox_w_out = nrm(ks[16], (N_FOX, D_MODEL, D_MODEL), D_MODEL ** -0.5)

    mlstm_w_in = nrm(ks[17], (N_MLSTM, D_MODEL, 2 * MLSTM_QK + 2 * D_MODEL + 2 * MLSTM_HEADS), D_MODEL ** -0.5)
    mlstm_b_gates = jnp.concatenate([
        MLSTM_IGATE_BIAS + nrm(ks[18], (N_MLSTM, MLSTM_HEADS), 0.1),
        MLSTM_FGATE_BIAS + nrm(ks[19], (N_MLSTM, MLSTM_HEADS), 0.1)], axis=-1)
    mlstm_norm_g = 1.0 + nrm(ks[20], (N_MLSTM, D_MODEL), 0.02)
    mlstm_w_out = nrm(ks[21], (N_MLSTM, D_MODEL, D_MODEL), D_MODEL ** -0.5)

    gdn_w_in = nrm(ks[22], (N_GDN, D_MODEL, GDN_CONV_DIM + GDN_V + 2 * GDN_V_HEADS), D_MODEL ** -0.5)
    gdn_conv_w = nrm(ks[23], (N_GDN, GDN_CONV, GDN_CONV_DIM), GDN_CONV ** -0.5)
    gdn_A_log = jnp.log(jax.random.uniform(ks[24], (N_GDN, GDN_V_HEADS), F32, 1.0, 16.0))
    gdn_dt_bias = nrm(ks[25], (N_GDN, GDN_V_HEADS), 0.1)
    gdn_norm_g = 1.0 + nrm(ks[26], (N_GDN, GDN_DV), 0.02)
    gdn_w_out = nrm(ks[27], (N_GDN, GDN_V, D_MODEL), GDN_V ** -0.5)

    return {'x_prompt': x_prompt, 'x_sample': x_sample,
            'cache_k': cache_k, 'cache_v': cache_v, 'cache_logf': cache_logf, 'page_table': page_table,
            'state_mlstm_C': state_mlstm_C, 'state_mlstm_n': state_mlstm_n, 'state_mlstm_m': state_mlstm_m,
            'state_gdn_S': state_gdn_S, 'state_gdn_conv': state_gdn_conv,
            'norm_gains': norm_gains, 'ffn_w_in': ffn_w_in, 'ffn_w_down': ffn_w_down,
            'fox_w_in': fox_w_in, 'fox_b_f': fox_b_f, 'fox_w_out': fox_w_out,
            'mlstm_w_in': mlstm_w_in, 'mlstm_b_gates': mlstm_b_gates, 'mlstm_norm_g': mlstm_norm_g,
            'mlstm_w_out': mlstm_w_out,
            'gdn_w_in': gdn_w_in, 'gdn_conv_w': gdn_conv_w, 'gdn_A_log': gdn_A_log, 'gdn_dt_bias': gdn_dt_bias,
            'gdn_norm_g': gdn_norm_g, 'gdn_w_out': gdn_w_out}


def reference(x_prompt, x_sample, cache_k, cache_v, cache_logf, page_table,
              state_mlstm_C, state_mlstm_n, state_mlstm_m, state_gdn_S, state_gdn_conv,
              norm_gains, ffn_w_in, ffn_w_down,
              fox_w_in, fox_b_f, fox_w_out,
              mlstm_w_in, mlstm_b_gates, mlstm_norm_g, mlstm_w_out,
              gdn_w_in, gdn_conv_w, gdn_A_log, gdn_dt_bias, gdn_norm_g, gdn_w_out):
    bp = x_prompt.shape[0]
    bs, ts = x_sample.shape[0], x_sample.shape[1]
    fkp, fvp, flp, fks, fvs, fls = [], [], [], [], [], []
    mCp, mnp, mmp, mCs, mns, mms = [], [], [], [], [], []
    gSp, gcp, gSs, gcs = [], [], [], []
    yp, ys = x_prompt, x_sample
    for i in range(DEPTH):
        kind, j = i % N_MIXERS, i // N_MIXERS
        g = norm_gains[i]
        yp = half_ffn(yp, g[0], g[1], ffn_w_in[i, 0], ffn_w_down[i, 0])
        ys = half_ffn(ys, g[0], g[1], ffn_w_in[i, 0], ffn_w_down[i, 0])
        hp, hs = rmsnorm(yp, g[2]), rmsnorm(ys, g[2])
        if kind == 0:
            qp, kp, vp, lp = fox_project(hp, fox_w_in[j], fox_b_f[j])
            mp = fox_attend_prompt(qp, kp, vp, lp) @ fox_w_out[j]
            qs, kq, vq, lq = fox_project(hs, fox_w_in[j], fox_b_f[j])
            k_past = cache_k[j, page_table].reshape(bs, -1, FOX_HEADS, FOX_HEAD_DIM)
            v_past = cache_v[j, page_table].reshape(bs, -1, FOX_HEADS, FOX_HEAD_DIM)
            l_past = cache_logf[j, page_table].reshape(bs, -1, FOX_HEADS)
            ms = fox_attend_sample(qs, kq, vq, lq, k_past, v_past, l_past) @ fox_w_out[j]
            fkp.append(kp); fvp.append(vp); flp.append(lp)
            fks.append(kq); fvs.append(vq); fls.append(lq)
        elif kind == 1:
            C0 = jnp.zeros((bp, MLSTM_HEADS, MLSTM_DQK, MLSTM_DV), F32)
            n0 = jnp.zeros((bp, MLSTM_HEADS, MLSTM_DQK), F32)
            m0 = jnp.zeros((bp, MLSTM_HEADS), F32)
            mp, Cp, np_, mp_ = mlstm_mixer(hp, mlstm_w_in[j], mlstm_b_gates[j], mlstm_norm_g[j], mlstm_w_out[j],
                                           C0, n0, m0, MLSTM_CHUNK)
            ms, Cs, ns_, ms_ = mlstm_mixer(hs, mlstm_w_in[j], mlstm_b_gates[j], mlstm_norm_g[j], mlstm_w_out[j],
                                           state_mlstm_C[j], state_mlstm_n[j], state_mlstm_m[j], ts)
            mCp.append(Cp); mnp.append(np_); mmp.append(mp_)
            mCs.append(Cs); mns.append(ns_); mms.append(ms_)
        else:
            S0 = jnp.zeros((bp, GDN_V_HEADS, GDN_DK, GDN_DV), F32)
            buf0 = jnp.zeros((bp, GDN_CONV - 1, GDN_CONV_DIM), hp.dtype)
            mp, Sp, bufp = gdn_mixer(hp, gdn_w_in[j], gdn_conv_w[j], gdn_A_log[j], gdn_dt_bias[j], gdn_norm_g[j],
                                     gdn_w_out[j], S0, buf0, GDN_CHUNK)
            ms, Ss, bufs = gdn_mixer(hs, gdn_w_in[j], gdn_conv_w[j], gdn_A_log[j], gdn_dt_bias[j], gdn_norm_g[j],
                                     gdn_w_out[j], state_gdn_S[j], state_gdn_conv[j], ts)
            gSp.append(Sp); gcp.append(bufp); gSs.append(Ss); gcs.append(bufs)
        yp = yp + rmsnorm(mp, g[3])
        ys = ys + rmsnorm(ms, g[3])
        yp = half_ffn(yp, g[4], g[5], ffn_w_in[i, 1], ffn_w_down[i, 1])
        ys = half_ffn(ys, g[4], g[5], ffn_w_in[i, 1], ffn_w_down[i, 1])
    return (yp, ys,
            jnp.stack(fkp), jnp.stack(fvp), jnp.stack(flp),
            jnp.stack(fks), jnp.stack(fvs), jnp.stack(fls),
            jnp.stack(mCp), jnp.stack(mnp), jnp.stack(mmp),
            jnp.stack(mCs), jnp.stack(mns), jnp.stack(mms),
            jnp.stack(gSp), jnp.stack(gcp), jnp.stack(gSs), jnp.stack(gcs))
```

```python
import functools

import jax
import jax.numpy as jnp
from jax import lax
from jax.experimental import pallas as pl
from jax.experimental.pallas import tpu as pltpu

F32 = jnp.float32
BF16 = jnp.bfloat16

EPS = 1e-6
LANES = 128
VMEM_LIMIT_BYTES = 56 * 1024 * 1024

FOX_HEADS = 16
FOX_HEAD_DIM = 128
PAGE_SIZE = 128
MLSTM_HEADS = 4
MLSTM_DQK = 256
MLSTM_DV = 512
GATE_SOFTCAP = 15.0
GDN_K_HEADS = 16
GDN_V_HEADS = 32
GDN_DK = 128
GDN_DV = 128
GDN_CONV = 4
GDN_CHUNK = 64
NEG_BIG = -1e30

FFN_TILE_F = 512
ROW_TILE = 512
MLSTM_PROMPT_CHUNK = 256
MLSTM_SAMPLE_ROWS = 128
GDN_ROW_BLOCK = 512
FLASH_TILE = 512


def _params(*sem):
    return pltpu.CompilerParams(dimension_semantics=sem, vmem_limit_bytes=VMEM_LIMIT_BYTES)


def _dot(a, b):
    return jnp.dot(a, b, preferred_element_type=F32)


def _dot_nt(a, b):
    return lax.dot_general(a, b, (((1,), (1,)), ((), ())), preferred_element_type=F32)


def _dot_tn(a, b):
    return lax.dot_general(a, b, (((0,), (0,)), ((), ())), preferred_element_type=F32)


def _split3(x):
    hi = x.astype(BF16)
    r1 = x - hi.astype(F32)
    mid = r1.astype(BF16)
    lo = (r1 - mid.astype(F32)).astype(BF16)
    return hi, mid, lo


def _select_nn(sel, x):
    hi, mid, lo = _split3(x)
    return _dot(sel, hi) + _dot(sel, mid) + _dot(sel, lo)


def _select_nt(sel, x):
    hi, mid, lo = _split3(x)
    return _dot_nt(sel, hi) + _dot_nt(sel, mid) + _dot_nt(sel, lo)


def _eye_bf16(n):
    r = lax.broadcasted_iota(jnp.int32, (n, n), 0)
    c = lax.broadcasted_iota(jnp.int32, (n, n), 1)
    return (r == c).astype(BF16)


def _rms(x, g):
    return x * lax.rsqrt(jnp.mean(x * x, axis=-1, keepdims=True) + EPS) * g


def _sigmoid(x):
    return 1.0 / (1.0 + jnp.exp(-x))


def _softplus(x):
    return jnp.maximum(x, 0.0) + jnp.log(1.0 + jnp.exp(-jnp.abs(x)))


def _log_sigmoid(x):
    return -_softplus(-x)


def _ffn_kernel(x_ref, gpre_ref, gpost_ref, wg_ref, wu_ref, wd_ref, o_ref, h_scr, acc_scr):
    j = pl.program_id(1)

    @pl.when(j == 0)
    def _():
        h_scr[...] = _rms(x_ref[...], gpre_ref[...]).astype(BF16)
        acc_scr[...] = jnp.zeros_like(acc_scr)

    h = h_scr[...]
    gate = _dot(h, wg_ref[...])
    up = _dot(h, wu_ref[...])
    act = (gate * _sigmoid(gate) * up).astype(BF16)
    acc_scr[...] += _dot(act, wd_ref[...])

    @pl.when(j == pl.num_programs(1) - 1)
    def _():
        o_ref[...] = x_ref[...] + 0.5 * _rms(acc_scr[...], gpost_ref[...])


def _half_ffn(x, g_pre, g_post, wg, wu, wd):
    m, d = x.shape
    tm = min(m, ROW_TILE)
    tf = FFN_TILE_F
    return pl.pallas_call(
        _ffn_kernel,
        grid=(m // tm, wg.shape[1] // tf),
        in_specs=[
            pl.BlockSpec((tm, d), lambda i, j: (i, 0)),
            pl.BlockSpec((1, d), lambda i, j: (0, 0)),
            pl.BlockSpec((1, d), lambda i, j: (0, 0)),
            pl.BlockSpec((d, tf), lambda i, j: (0, j)),
            pl.BlockSpec((d, tf), lambda i, j: (0, j)),
            pl.BlockSpec((tf, d), lambda i, j: (j, 0)),
        ],
        out_specs=pl.BlockSpec((tm, d), lambda i, j: (i, 0)),
        out_shape=jax.ShapeDtypeStruct((m, d), F32),
        scratch_shapes=[pltpu.VMEM((tm, d), BF16), pltpu.VMEM((tm, d), F32)],
        compiler_params=_params("parallel", "arbitrary"),
        name="half_ffn",
    )(x, g_pre, g_post, wg, wu, wd)


def _ffn_weights(w_in, w_down):
    d_ff = w_down.shape[0]
    pad = (-d_ff) % FFN_TILE_F
    wg = jnp.pad(w_in[:, :d_ff].astype(BF16), ((0, 0), (0, pad)))
    wu = jnp.pad(w_in[:, d_ff:].astype(BF16), ((0, 0), (0, pad)))
    wd = jnp.pad(w_down.astype(BF16), ((0, pad), (0, 0)))
    return wg, wu, wd


def _norm_matmul_kernel(x_ref, g_ref, w_ref, o_ref, h_scr):
    @pl.when(pl.program_id(1) == 0)
    def _():
        h_scr[...] = _rms(x_ref[...], g_ref[...]).astype(BF16)

    o_ref[...] = _dot(h_scr[...], w_ref[...]).astype(o_ref.dtype)


def _norm_matmul(x, g, w):
    m, d = x.shape
    n = w.shape[1]
    tm = min(m, ROW_TILE)
    tn = 512 if n % 512 == 0 else (256 if n % 256 == 0 else LANES)
    return pl.pallas_call(
        _norm_matmul_kernel,
        grid=(m // tm, n // tn),
        in_specs=[
            pl.BlockSpec((tm, d), lambda i, j: (i, 0)),
            pl.BlockSpec((1, d), lambda i, j: (0, 0)),
            pl.BlockSpec((d, tn), lambda i, j: (0, j)),
        ],
        out_specs=pl.BlockSpec((tm, tn), lambda i, j: (i, j)),
        out_shape=jax.ShapeDtypeStruct((m, n), F32),
        scratch_shapes=[pltpu.VMEM((tm, d), BF16)],
        compiler_params=_params("parallel", "arbitrary"),
        name="norm_matmul",
    )(x, g, w)


def _pad_cols(w, n):
    return jnp.pad(w, ((0, 0), (0, n - w.shape[1])))


def _proj_res_kernel(a_ref, w_ref, g_ref, x_ref, o_ref, acc_scr):
    k = pl.program_id(1)

    @pl.when(k == 0)
    def _():
        acc_scr[...] = jnp.zeros_like(acc_scr)

    acc_scr[...] += _dot(a_ref[...], w_ref[...])

    @pl.when(k == pl.num_programs(1) - 1)
    def _():
        o_ref[...] = x_ref[...] + _rms(acc_scr[...], g_ref[...])


def _proj_res(a, w, g, x):
    m, kdim = a.shape
    d = w.shape[1]
    tm = min(m, ROW_TILE)
    tk = min(kdim, 2048)
    return pl.pallas_call(
        _proj_res_kernel,
        grid=(m // tm, kdim // tk),
        in_specs=[
            pl.BlockSpec((tm, tk), lambda i, k: (i, k)),
            pl.BlockSpec((tk, d), lambda i, k: (k, 0)),
            pl.BlockSpec((1, d), lambda i, k: (0, 0)),
            pl.BlockSpec((tm, d), lambda i, k: (i, 0)),
        ],
        out_specs=pl.BlockSpec((tm, d), lambda i, k: (i, 0)),
        out_shape=jax.ShapeDtypeStruct((m, d), F32),
        scratch_shapes=[pltpu.VMEM((tm, d), F32)],
        compiler_params=_params("parallel", "arbitrary"),
        name="proj_res",
    )(a, w, g, x)


def _fox_gate_kernel(gl_ref, b_ref, logf_ref, c_ref):
    t = gl_ref.shape[0]
    lf = _log_sigmoid(gl_ref[...] + b_ref[...])
    logf_ref[...] = lf[:, :FOX_HEADS]
    c = lf.T[:FOX_HEADS, :]
    lane = lax.broadcasted_iota(jnp.int32, c.shape, 1)
    shift = 1
    while shift < t:
        c = c + jnp.where(lane >= shift, pltpu.roll(c, shift, 1), 0.0)
        shift *= 2
    c_ref[...] = c


def _fox_gates(gl, bias):
    b, t, _ = gl.shape
    return pl.pallas_call(
        _fox_gate_kernel,
        grid=(b,),
        in_specs=[
            pl.BlockSpec((None, t, LANES), lambda i: (i, 0, 0)),
            pl.BlockSpec((1, LANES), lambda i: (0, 0)),
        ],
        out_specs=[
            pl.BlockSpec((None, t, FOX_HEADS), lambda i: (i, 0, 0)),
            pl.BlockSpec((None, FOX_HEADS, t), lambda i: (i, 0, 0)),
        ],
        out_shape=[
            jax.ShapeDtypeStruct((b, t, FOX_HEADS), F32),
            jax.ShapeDtypeStruct((b, FOX_HEADS, t), F32),
        ],
        compiler_params=_params("parallel"),
        name="fox_gates",
    )(gl, bias)


def _fox_flash_kernel(q_ref, k_ref, v_ref, c_ref, o_ref, m_scr, l_scr, acc_scr, *, scale):
    qi = pl.program_id(2)
    ki = pl.program_id(3)

    @pl.when(ki == 0)
    def _():
        m_scr[...] = jnp.full_like(m_scr, -jnp.inf)
        l_scr[...] = jnp.zeros_like(l_scr)
        acc_scr[...] = jnp.zeros_like(acc_scr)

    def step(masked):
        q = (q_ref[...] * scale).astype(BF16)
        s = _dot_nt(q, k_ref[...].astype(BF16)) - c_ref[...]
        if masked:
            r = lax.broadcasted_iota(jnp.int32, s.shape, 0)
            c = lax.broadcasted_iota(jnp.int32, s.shape, 1)
            s = jnp.where(c <= r, s, -jnp.inf)
        m_old = m_scr[...]
        m_new = jnp.maximum(m_old, jnp.max(s, axis=-1, keepdims=True))
        alpha = jnp.exp(m_old - m_new)
        p = jnp.exp(s - m_new)
        l_scr[...] = alpha * l_scr[...] + jnp.sum(p, axis=-1, keepdims=True)
        acc_scr[...] = alpha * acc_scr[...] + _dot(p.astype(BF16), v_ref[...].astype(BF16))
        m_scr[...] = m_new

    @pl.when(ki < qi)
    def _():
        step(False)

    @pl.when(ki == qi)
    def _():
        step(True)
        o_ref[...] = (acc_scr[...] / l_scr[...]).astype(o_ref.dtype)


def _fox_flash(q, k, v, c):
    b, t, _ = q.shape
    tq = min(t, FLASH_TILE)
    nq = t // tq
    dh = FOX_HEAD_DIM
    kv_spec = pl.BlockSpec((None, tq, dh), lambda bi, h, qi, ki: (bi, jnp.minimum(ki, qi), h))
    return pl.pallas_call(
        functools.partial(_fox_flash_kernel, scale=dh ** -0.5),
        grid=(b, FOX_HEADS, nq, nq),
        in_specs=[
            pl.BlockSpec((None, tq, dh), lambda bi, h, qi, ki: (bi, qi, h)),
            kv_spec,
            kv_spec,
            pl.BlockSpec((None, None, 1, tq), lambda bi, h, qi, ki: (bi, h, 0, jnp.minimum(ki, qi))),
        ],
        out_specs=pl.BlockSpec((None, tq, dh), lambda bi, h, qi, ki: (bi, qi, h)),
        out_shape=jax.ShapeDtypeStruct(q.shape, BF16),
        scratch_shapes=[pltpu.VMEM((tq, 1), F32), pltpu.VMEM((tq, 1), F32), pltpu.VMEM((tq, dh), F32)],
        compiler_params=_params("parallel", "parallel", "parallel", "arbitrary"),
        name="fox_flash",
    )(q, k, v, c)


def _fox_cpast_kernel(pt_ref, lf_ref, o_ref, carry_scr):
    del pt_ref

    @pl.when(pl.program_id(1) == 0)
    def _():
        carry_scr[...] = jnp.zeros_like(carry_scr)

    lf = lf_ref[...]
    n = lf.shape[1]
    lane = lax.broadcasted_iota(jnp.int32, lf.shape, 1)
    incl = lf
    shift = 1
    while shift < n:
        incl = incl + jnp.where(lane + shift < n, pltpu.roll(incl, n - shift, 1), 0.0)
        shift *= 2
    carry = carry_scr[...]
    o_ref[...] = -((incl - lf) + carry)
    carry_scr[...] = carry + incl[:, 0:1]


def _fox_cpast(page_table, logf_t, layer):
    b, n_pages = page_table.shape
    heads, rows = logf_t.shape[2], logf_t.shape[3]
    grid_spec = pltpu.PrefetchScalarGridSpec(
        num_scalar_prefetch=1,
        grid=(b, n_pages),
        in_specs=[pl.BlockSpec((None, None, heads, rows),
                               lambda bi, p, pt: (layer, pt[bi, n_pages - 1 - p], 0, 0))],
        out_specs=pl.BlockSpec((None, None, heads, rows), lambda bi, p, pt: (bi, n_pages - 1 - p, 0, 0)),
        scratch_shapes=[pltpu.VMEM((heads, rows), F32)],
    )
    return pl.pallas_call(
        _fox_cpast_kernel,
        grid_spec=grid_spec,
        out_shape=jax.ShapeDtypeStruct((b, n_pages, heads, rows), F32),
        compiler_params=_params("parallel", "arbitrary"),
        name="fox_cpast",
    )(page_table, logf_t)


def _fox_decode_kernel(pt_ref, q_ref, kn_ref, vn_ref, cn_ref, cp_ref, kc_ref, vc_ref, o_ref,
                       qb_scr, m_scr, l_scr, acc_scr, *, scale, t_new):
    del pt_ref
    p = pl.program_id(1)
    heads = FOX_HEADS
    nrow = t_new * heads
    width = heads * FOX_HEAD_DIM
    row = lax.broadcasted_iota(jnp.int32, (nrow, width), 0)
    col = lax.broadcasted_iota(jnp.int32, (nrow, width), 1)
    head_mask = (row % heads) == (col // FOX_HEAD_DIM)

    @pl.when(p == 0)
    def _():
        q = q_ref[...] * scale
        qrep = jnp.concatenate([jnp.broadcast_to(q[t:t + 1, :], (heads, width)) for t in range(t_new)], axis=0)
        qb_scr[...] = jnp.where(head_mask, qrep, 0.0).astype(BF16)
        m_scr[...] = jnp.full_like(m_scr, -jnp.inf)
        l_scr[...] = jnp.zeros_like(l_scr)
        acc_scr[...] = jnp.zeros_like(acc_scr)

    def update(s, v_rows):
        m_old = m_scr[...]
        m_new = jnp.maximum(m_old, jnp.max(s, axis=-1, keepdims=True))
        alpha = jnp.exp(m_old - m_new)
        pe = jnp.exp(s - m_new)
        l_scr[...] = alpha * l_scr[...] + jnp.sum(pe, axis=-1, keepdims=True)
        acc_scr[...] = alpha * acc_scr[...] + _dot(pe.astype(BF16), v_rows.astype(BF16))
        m_scr[...] = m_new

    s_past = _dot_nt(qb_scr[...], kc_ref[...].astype(BF16))
    s_past = s_past - jnp.concatenate([cp_ref[...]] * t_new, axis=0)
    update(s_past, vc_ref[...])

    @pl.when(p == pl.num_programs(1) - 1)
    def _():
        s_new = _dot_nt(qb_scr[...], kn_ref[...].astype(BF16))
        s_new = s_new - jnp.concatenate([cn_ref[...]] * t_new, axis=0)
        r = lax.broadcasted_iota(jnp.int32, s_new.shape, 0) // heads
        c = lax.broadcasted_iota(jnp.int32, s_new.shape, 1)
        s_new = jnp.where(c <= r, s_new, -jnp.inf)
        update(s_new, vn_ref[...])
        out = jnp.where(head_mask, acc_scr[...] / l_scr[...], 0.0)
        o_ref[...] = jnp.concatenate(
            [jnp.sum(out[t * heads:(t + 1) * heads, :], axis=0, keepdims=True) for t in range(t_new)], axis=0)


def _fox_decode(page_table, q, k_new, v_new, c_new, c_past, cache_k, cache_v, layer):
    b, t_new, width = q.shape
    n_pages = page_table.shape[1]
    rows_pad = k_new.shape[1]
    page = cache_k.shape[2]
    nrow = t_new * FOX_HEADS
    cache_spec = pl.BlockSpec((None, None, page, width), lambda bi, p, pt: (layer, pt[bi, p], 0, 0))
    grid_spec = pltpu.PrefetchScalarGridSpec(
        num_scalar_prefetch=1,
        grid=(b, n_pages),
        in_specs=[
            pl.BlockSpec((None, t_new, width), lambda bi, p, pt: (bi, 0, 0)),
            pl.BlockSpec((None, rows_pad, width), lambda bi, p, pt: (bi, 0, 0)),
            pl.BlockSpec((None, rows_pad, width), lambda bi, p, pt: (bi, 0, 0)),
            pl.BlockSpec((None, FOX_HEADS, rows_pad), lambda bi, p, pt: (bi, 0, 0)),
            pl.BlockSpec((None, None, FOX_HEADS, page), lambda bi, p, pt: (bi, p, 0, 0)),
            cache_spec,
            cache_spec,
        ],
        out_specs=pl.BlockSpec((None, t_new, width), lambda bi, p, pt: (bi, 0, 0)),
        scratch_shapes=[
            pltpu.VMEM((nrow, width), BF16),
            pltpu.VMEM((nrow, 1), F32),
            pltpu.VMEM((nrow, 1), F32),
            pltpu.VMEM((nrow, width), F32),
        ],
    )
    return pl.pallas_call(
        functools.partial(_fox_decode_kernel, scale=FOX_HEAD_DIM ** -0.5, t_new=t_new),
        grid_spec=grid_spec,
        out_shape=jax.ShapeDtypeStruct((b, t_new, width), F32),
        compiler_params=_params("parallel", "arbitrary"),
        name="fox_decode",
    )(page_table, q, k_new, v_new, c_new, c_past, cache_k, cache_v)


def _mlstm_kernel(q_ref, k_ref, v_ref, o_ref, gl_ref, bias_ref, ng_ref, c0_ref, n0_ref, m0_ref,
                  h_ref, c_ref, n_ref, m_ref, *, valid_len):
    ci = pl.program_id(1)
    rows = q_ref.shape[0]
    scale = MLSTM_DQK ** -0.5

    @pl.when(ci == 0)
    def _():
        c_ref[...] = c0_ref[...]
        n_ref[...] = n0_ref[...]
        m_ref[...] = m0_ref[...]

    gates = gl_ref[...] + bias_ref[...]
    gates = GATE_SOFTCAP * jnp.tanh(gates / GATE_SOFTCAP)
    logf = _log_sigmoid(gates)
    ig = gates
    if valid_len is not None:
        grow = ci * rows + lax.broadcasted_iota(jnp.int32, (rows, 1), 0)
        ig = jnp.where(grow < valid_len, ig, NEG_BIG)
        logf = jnp.where(grow < valid_len, logf, 0.0)

    r = lax.broadcasted_iota(jnp.int32, (rows, rows), 0)
    c = lax.broadcasted_iota(jnp.int32, (rows, rows), 1)
    causal = c <= r
    b_cols = _select_nn(causal.astype(BF16), logf)
    eye = _eye_bf16(LANES)
    b_rows = _select_nt(eye, b_cols)
    i_rows = _select_nt(eye, ig)
    lane = lax.broadcasted_iota(jnp.int32, (1, LANES), 1)
    m_all = m_ref[...]
    m_next = m_all

    for h in range(MLSTM_HEADS):
        fh = MLSTM_HEADS + h
        qh = q_ref[:, h * MLSTM_DQK:(h + 1) * MLSTM_DQK] * scale
        kh = k_ref[:, h * MLSTM_DQK:(h + 1) * MLSTM_DQK]
        vh = v_ref[:, h * MLSTM_DV:(h + 1) * MLSTM_DV]
        qb = qh.astype(BF16)
        vb = vh.astype(BF16)
        b_col = b_cols[:, fh:fh + 1]
        i_col = ig[:, h:h + 1]
        b_row = b_rows[fh:fh + 1, :]
        i_row = i_rows[h:h + 1, :]
        m_prev = m_all[:, h:h + 1]
        c_state = c_ref[h]
        n_state = n_ref[h:h + 1, :]

        log_d = jnp.where(causal, b_col - b_row + i_row, -jnp.inf)
        log_inter = b_col + m_prev
        m_t = jnp.maximum(log_inter, jnp.max(log_d, axis=-1, keepdims=True))
        d = jnp.exp(log_d - m_t)
        a = jnp.exp(log_inter - m_t)
        s = _dot_nt(qb, kh.astype(BF16)) * d
        num = _dot(s.astype(BF16), vb) + a * _dot(qb, c_state.astype(BF16))
        den = jnp.sum(s, axis=-1, keepdims=True) + a * jnp.sum(qh * n_state, axis=-1, keepdims=True)
        ht = num / jnp.maximum(jnp.abs(den), jnp.exp(-m_t))

        b_last = b_row[:, rows - 1:rows]
        m_new = jnp.maximum(b_last + m_prev, jnp.max(b_last - b_row + i_row, axis=-1, keepdims=True))
        w_col = jnp.exp(b_last - b_col + i_col - m_new)
        decay = jnp.exp(b_last + m_prev - m_new)
        kw = kh * w_col
        c_ref[h] = decay * c_state + _dot_tn(kw.astype(BF16), vb)
        n_ref[h:h + 1, :] = decay * n_state + jnp.sum(kw, axis=0, keepdims=True)
        m_next = jnp.where(lane == h, m_new, m_next)

        og = o_ref[:, h * MLSTM_DV:(h + 1) * MLSTM_DV]
        hn = _rms(ht, ng_ref[:, h * MLSTM_DV:(h + 1) * MLSTM_DV])
        h_ref[:, h * MLSTM_DV:(h + 1) * MLSTM_DV] = (_sigmoid(og) * hn).astype(h_ref.dtype)

    m_ref[...] = m_next


def _mlstm_scan(q, k, v, o, gl, bias, norm_g, c0, n0, m0, rows, valid_len):
    b, t, _ = q.shape
    dm = v.shape[2]
    qk = q.shape[2]
    seq = lambda w: pl.BlockSpec((None, rows, w), lambda bi, ci: (bi, ci, 0))
    state_specs = [
        pl.BlockSpec((None, MLSTM_HEADS, MLSTM_DQK, MLSTM_DV), lambda bi, ci: (bi, 0, 0, 0)),
        pl.BlockSpec((None, MLSTM_HEADS, MLSTM_DQK), lambda bi, ci: (bi, 0, 0)),
        pl.BlockSpec((None, 1, LANES), lambda bi, ci: (bi, 0, 0)),
    ]
    return pl.pallas_call(
        functools.partial(_mlstm_kernel, valid_len=valid_len),
        grid=(b, t // rows),
        in_specs=[seq(qk), seq(qk), seq(dm), seq(dm), seq(LANES),
                  pl.BlockSpec((1, LANES), lambda bi, ci: (0, 0)),
                  pl.BlockSpec((1, dm), lambda bi, ci: (0, 0))] + state_specs,
        out_specs=[seq(dm)] + state_specs,
        out_shape=[
            jax.ShapeDtypeStruct((b, t, dm), BF16),
            jax.ShapeDtypeStruct(c0.shape, F32),
            jax.ShapeDtypeStruct(n0.shape, F32),
            jax.ShapeDtypeStruct(m0.shape, F32),
        ],
        compiler_params=_params("parallel", "arbitrary"),
        name="mlstm_scan",
    )(q, k, v, o, gl, bias, norm_g, c0, n0, m0)


def _gdn_conv_kernel(x_ref, buf_ref, w_ref, o_ref, ext_scr):
    ci = pl.program_id(1)
    ti = pl.program_id(2)
    tt = x_ref.shape[0]
    halo = 8

    @pl.when(ti == 0)
    def _():
        ext_scr[0:halo, :] = buf_ref[...]

    ext_scr[halo:halo + tt, :] = x_ref[...]
    acc = ext_scr[halo:halo + tt, :] * w_ref[GDN_CONV - 1:GDN_CONV, :]
    for j in range(GDN_CONV - 1):
        off = halo - (GDN_CONV - 1) + j
        acc = acc + ext_scr[off:off + tt, :] * w_ref[j:j + 1, :]
    tail = ext_scr[tt:tt + halo, :]
    ext_scr[0:halo, :] = tail
    y = acc * _sigmoid(acc)

    n_qk_tiles = (2 * GDN_K_HEADS * GDN_DK) // x_ref.shape[1]

    @pl.when(ci < n_qk_tiles)
    def _():
        for h in range(x_ref.shape[1] // GDN_DK):
            yh = y[:, h * GDN_DK:(h + 1) * GDN_DK]
            o_ref[:, h * GDN_DK:(h + 1) * GDN_DK] = yh * lax.rsqrt(jnp.sum(yh * yh, axis=-1, keepdims=True) + EPS)

    @pl.when(ci >= n_qk_tiles)
    def _():
        o_ref[...] = y


def _gdn_conv(mixed, buf8, conv_w):
    b, t, c = mixed.shape
    tt = min(t, 512)
    tc = 2048
    return pl.pallas_call(
        _gdn_conv_kernel,
        grid=(b, c // tc, t // tt),
        in_specs=[
            pl.BlockSpec((None, tt, tc), lambda bi, ci, ti: (bi, ti, ci)),
            pl.BlockSpec((None, 8, tc), lambda bi, ci, ti: (bi, 0, ci)),
            pl.BlockSpec((GDN_CONV, tc), lambda bi, ci, ti: (0, ci)),
        ],
        out_specs=pl.BlockSpec((None, tt, tc), lambda bi, ci, ti: (bi, ti, ci)),
        out_shape=jax.ShapeDtypeStruct(mixed.shape, F32),
        scratch_shapes=[pltpu.VMEM((tt + 8, tc), F32)],
        compiler_params=_params("parallel", "parallel", "arbitrary"),
        name="gdn_conv",
    )(mixed, buf8, conv_w)


def _gdn_gate_kernel(gl_ref, a_ref, dt_ref, beta_ref, gcum_ref, gcum_t_ref, *, valid_len):
    rows = gl_ref.shape[0]
    gl = gl_ref[...]
    beta = _sigmoid(gl)
    g = -jnp.exp(a_ref[...]) * _softplus(gl + dt_ref[...])
    if valid_len is not None:
        grow = pl.program_id(1) * rows + lax.broadcasted_iota(jnp.int32, (rows, 1), 0)
        beta = jnp.where(grow < valid_len, beta, 0.0)
        g = jnp.where(grow < valid_len, g, 0.0)
    beta_ref[...] = beta
    r = lax.broadcasted_iota(jnp.int32, (GDN_CHUNK, GDN_CHUNK), 0)
    c = lax.broadcasted_iota(jnp.int32, (GDN_CHUNK, GDN_CHUNK), 1)
    tri = (c <= r).astype(BF16)
    eye = _eye_bf16(LANES)
    for j in range(rows // GDN_CHUNK):
        gc = _select_nn(tri, g[j * GDN_CHUNK:(j + 1) * GDN_CHUNK, :])
        gcum_ref[j * GDN_CHUNK:(j + 1) * GDN_CHUNK, :] = gc
        gcum_t_ref[j] = _select_nt(eye, gc)


def _gdn_gates(gl, a_pad, dt_pad, valid_len):
    b, t, _ = gl.shape
    rows = min(t, GDN_ROW_BLOCK)
    nsub = rows // GDN_CHUNK
    seq = pl.BlockSpec((None, rows, LANES), lambda bi, ci: (bi, ci, 0))
    vec = pl.BlockSpec((1, LANES), lambda bi, ci: (0, 0))
    return pl.pallas_call(
        functools.partial(_gdn_gate_kernel, valid_len=valid_len),
        grid=(b, t // rows),
        in_specs=[seq, vec, vec],
        out_specs=[seq, seq, pl.BlockSpec((None, nsub, LANES, GDN_CHUNK), lambda bi, ci: (bi, ci, 0, 0))],
        out_shape=[
            jax.ShapeDtypeStruct((b, t, LANES), F32),
            jax.ShapeDtypeStruct((b, t, LANES), F32),
            jax.ShapeDtypeStruct((b, t // GDN_CHUNK, LANES, GDN_CHUNK), F32),
        ],
        compiler_params=_params("parallel", "parallel"),
        name="gdn_gates",
    )(gl, a_pad, dt_pad)


def _unit_lower_inverse_minus_eye(a):
    n = a.shape[0]
    r = lax.broadcasted_iota(jnp.int32, (n, n), 0)
    c = lax.broadcasted_iota(jnp.int32, (n, n), 1)
    blk16 = (r // 16) == (c // 16)
    blk32 = (r // 32) == (c // 32)

    def mm(x, y):
        return _dot(x.astype(BF16), y.astype(BF16))

    nk = jnp.where(blk16, -a, 0.0)
    x = nk
    for _ in range(3):
        nk = mm(nk, nk)
        x = x + nk + mm(x, nk)
    for off in (jnp.where(blk32 & ~blk16, a, 0.0), jnp.where(~blk32, a, 0.0)):
        y = off + mm(x, off)
        x = x - (y + mm(y, x))
    return x


def _gdn_chunk_kernel(q_ref, k_ref, v_ref, z_ref, beta_ref, gcum_ref, gcum_t_ref, ng_ref, s0_ref,
                      o_ref, s_ref):
    kh = pl.program_id(1)
    ti = pl.program_id(2)
    n_chunks = q_ref.shape[0] // GDN_CHUNK
    scale = GDN_DK ** -0.5
    rep = GDN_V_HEADS // GDN_K_HEADS

    @pl.when(ti == 0)
    def _():
        s_ref[...] = s0_ref[...]

    L = GDN_CHUNK
    r = lax.broadcasted_iota(jnp.int32, (L, L), 0)
    c = lax.broadcasted_iota(jnp.int32, (L, L), 1)
    causal = c <= r
    strict = c < r
    lane = lax.broadcasted_iota(jnp.int32, (1, LANES), 1)
    sub = lax.broadcasted_iota(jnp.int32, (LANES, 1), 0)

    def chunk(j, carry):
        r0 = pl.multiple_of(j * L, L)
        q = q_ref[pl.ds(r0, L), :] * scale
        k = k_ref[pl.ds(r0, L), :]
        kb = k.astype(BF16)
        kk = _dot_nt(kb, kb)
        qk = _dot_nt(q.astype(BF16), kb)
        beta_blk = beta_ref[pl.ds(r0, L), :]
        gc_blk = gcum_ref[pl.ds(r0, L), :]
        gc_t = gcum_t_ref[j]
        for e in range(rep):
            h = kh * rep + e
            bt = jnp.sum(jnp.where(lane == h, beta_blk, 0.0), axis=1, keepdims=True)
            gc = jnp.sum(jnp.where(lane == GDN_V_HEADS + h, gc_blk, 0.0), axis=1, keepdims=True)
            gc_row = jnp.sum(jnp.where(sub == GDN_V_HEADS + h, gc_t, 0.0), axis=0, keepdims=True)
            g_last = gc_row[:, L - 1:L]
            decay = jnp.exp(jnp.where(causal, gc - gc_row, -jnp.inf))
            a = jnp.where(strict, bt * kk * decay, 0.0)
            tm1 = _unit_lower_inverse_minus_eye(a).astype(BF16)
            vbeta = v_ref[pl.ds(r0, L), e * GDN_DV:(e + 1) * GDN_DV] * bt
            kbeta = k * (bt * jnp.exp(gc))
            u = vbeta + _dot(tm1, vbeta.astype(BF16))
            w = kbeta + _dot(tm1, kbeta.astype(BF16))
            s_state = s_ref[e]
            sb = s_state.astype(BF16)
            v_new = u - _dot(w.astype(BF16), sb)
            attn = jnp.where(causal, qk * decay, 0.0)
            out = _dot((q * jnp.exp(gc)).astype(BF16), sb) + _dot(attn.astype(BF16), v_new.astype(BF16))
            k_dec = (k * jnp.exp(g_last - gc)).astype(BF16)
            s_ref[e] = s_state * jnp.exp(g_last) + _dot_tn(k_dec, v_new.astype(BF16))
            zz = z_ref[pl.ds(r0, L), e * GDN_DV:(e + 1) * GDN_DV]
            core = _rms(out, ng_ref[...]) * (zz * _sigmoid(zz))
            o_ref[pl.ds(r0, L), e * GDN_DV:(e + 1) * GDN_DV] = core.astype(o_ref.dtype)
        return carry

    lax.fori_loop(0, n_chunks, chunk, 0)


def _gdn_scan(conv, z, beta, gcum, gcum_t, norm_g, s0):
    b, t, _ = conv.shape
    rows = min(t, GDN_ROW_BLOCK)
    rep = GDN_V_HEADS // GDN_K_HEADS
    vw = rep * GDN_DV
    k_off = GDN_K_HEADS
    v_off = (2 * GDN_K_HEADS * GDN_DK) // vw
    state_spec = pl.BlockSpec((None, rep, GDN_DK, GDN_DV), lambda bi, kh, ti: (bi, kh, 0, 0))
    gate_spec = pl.BlockSpec((None, rows, LANES), lambda bi, kh, ti: (bi, ti, 0))
    return pl.pallas_call(
        _gdn_chunk_kernel,
        grid=(b, GDN_K_HEADS, t // rows),
        in_specs=[
            pl.BlockSpec((None, rows, GDN_DK), lambda bi, kh, ti: (bi, ti, kh)),
            pl.BlockSpec((None, rows, GDN_DK), lambda bi, kh, ti: (bi, ti, k_off + kh)),
            pl.BlockSpec((None, rows, vw), lambda bi, kh, ti: (bi, ti, v_off + kh)),
            pl.BlockSpec((None, rows, vw), lambda bi, kh, ti: (bi, ti, kh)),
            gate_spec,
            gate_spec,
            pl.BlockSpec((None, rows // GDN_CHUNK, LANES, GDN_CHUNK), lambda bi, kh, ti: (bi, ti, 0, 0)),
            pl.BlockSpec((1, GDN_DV), lambda bi, kh, ti: (0, 0)),
            state_spec,
        ],
        out_specs=[pl.BlockSpec((None, rows, vw), lambda bi, kh, ti: (bi, ti, kh)), state_spec],
        out_shape=[jax.ShapeDtypeStruct(z.shape, BF16), jax.ShapeDtypeStruct(s0.shape, F32)],
        compiler_params=_params("parallel", "parallel", "arbitrary"),
        name="gdn_scan",
    )(conv, conv, conv, z, beta, gcum, gcum_t, norm_g, s0)


def _pad_rows(x, rows):
    return jnp.pad(x, ((0, 0), (0, rows - x.shape[1]), (0, 0)))


def _fox_layer(yp, ys, bp, bs, g_pre, g_post, w_in, b_f, w_out, cache_k, cache_v, logf_t, page_table, layer):
    d = yp.shape[1]
    ts = ys.shape[0] // bs
    wb = w_in.astype(BF16)
    w_q, w_k, w_v = wb[:, :d], wb[:, d:2 * d], wb[:, 2 * d:3 * d]
    w_g = _pad_cols(wb[:, 3 * d:], LANES)
    bias = jnp.pad(b_f, (0, LANES - FOX_HEADS)).reshape(1, LANES)
    wo = w_out.astype(BF16)

    q, k, v = (_norm_matmul(yp, g_pre, w) for w in (w_q, w_k, w_v))
    gl = _norm_matmul(yp, g_pre, w_g)
    tp = yp.shape[0] // bp
    logf_p, c_p = _fox_gates(gl.reshape(bp, tp, LANES), bias)
    attn = _fox_flash(q.reshape(bp, tp, d), k.reshape(bp, tp, d), v.reshape(bp, tp, d),
                      c_p.reshape(bp, FOX_HEADS, 1, tp))
    yp = _proj_res(attn.reshape(bp * tp, d), wo, g_post, yp)
    prompt_out = (k.reshape(bp, tp, FOX_HEADS, FOX_HEAD_DIM), v.reshape(bp, tp, FOX_HEADS, FOX_HEAD_DIM), logf_p)

    qs, ks, vs = (_norm_matmul(ys, g_pre, w) for w in (w_q, w_k, w_v))
    gls = _norm_matmul(ys, g_pre, w_g)
    logf_s, c_s = _fox_gates(_pad_rows(gls.reshape(bs, ts, LANES), LANES), bias)
    logf_s = logf_s[:, :ts]
    c_past = _fox_cpast(page_table, logf_t, layer)
    attn_s = _fox_decode(page_table, qs.reshape(bs, ts, d),
                         _pad_rows(ks.reshape(bs, ts, d), LANES), _pad_rows(vs.reshape(bs, ts, d), LANES),
                         c_s, c_past, cache_k, cache_v, layer)
    ys = _proj_res(attn_s.reshape(bs * ts, d).astype(BF16), wo, g_post, ys)
    sample_out = (ks.reshape(bs, ts, FOX_HEADS, FOX_HEAD_DIM), vs.reshape(bs, ts, FOX_HEADS, FOX_HEAD_DIM), logf_s)
    return yp, ys, prompt_out, sample_out


def _mlstm_layer(yp, ys, bp, bs, g_pre, g_post, w_in, b_gates, norm_g, w_out, c0s, n0s, m0s):
    d = yp.shape[1]
    hq = MLSTM_HEADS * MLSTM_DQK
    wb = w_in.astype(BF16)
    w_q, w_k = wb[:, :hq], wb[:, hq:2 * hq]
    w_v, w_o = wb[:, 2 * hq:2 * hq + d], wb[:, 2 * hq + d:2 * hq + 2 * d]
    w_g = _pad_cols(wb[:, 2 * hq + 2 * d:], LANES)
    bias = jnp.pad(b_gates, (0, LANES - 2 * MLSTM_HEADS)).reshape(1, LANES)
    wo = w_out.astype(BF16)
    ng = norm_g.reshape(1, d)

    def run(y, b, c0, n0, m0, rows, pad_to, valid_len):
        t = y.shape[0] // b
        parts = [_norm_matmul(y, g_pre, w).reshape(b, t, -1) for w in (w_q, w_k, w_v, w_o, w_g)]
        if pad_to is not None:
            parts = [_pad_rows(p, pad_to) for p in parts]
        m0p = jnp.pad(m0, ((0, 0), (0, LANES - MLSTM_HEADS))).reshape(b, 1, LANES)
        hn, c_fin, n_fin, m_fin = _mlstm_scan(*parts, bias, ng, c0, n0, m0p, rows, valid_len)
        hn = hn[:, :t].reshape(b * t, d)
        return _proj_res(hn, wo, g_post, y), c_fin, n_fin, m_fin[:, 0, :MLSTM_HEADS]

    zc = jnp.zeros((bp, MLSTM_HEADS, MLSTM_DQK, MLSTM_DV), F32)
    zn = jnp.zeros((bp, MLSTM_HEADS, MLSTM_DQK), F32)
    zm = jnp.zeros((bp, MLSTM_HEADS), F32)
    yp, cp, np_, mp = run(yp, bp, zc, zn, zm, MLSTM_PROMPT_CHUNK, None, None)
    ts = ys.shape[0] // bs
    ys, cs, ns, ms = run(ys, bs, c0s.astype(F32), n0s.astype(F32), m0s.astype(F32),
                         MLSTM_SAMPLE_ROWS, MLSTM_SAMPLE_ROWS, ts)
    return yp, ys, (cp, np_, mp), (cs, ns, ms)


def _gdn_layer(yp, ys, bp, bs, g_pre, g_post, w_in, conv_w, a_log, dt_bias, norm_g, w_out, s0s, bufs):
    conv_dim = 2 * GDN_K_HEADS * GDN_DK + GDN_V_HEADS * GDN_DV
    vdim = GDN_V_HEADS * GDN_DV
    wb = w_in.astype(BF16)
    w_mix, w_z = wb[:, :conv_dim], wb[:, conv_dim:conv_dim + vdim]
    w_g = _pad_cols(wb[:, conv_dim + vdim:], LANES)
    a_pad = jnp.pad(a_log, (GDN_V_HEADS, LANES - 2 * GDN_V_HEADS)).reshape(1, LANES)
    dt_pad = jnp.pad(dt_bias, (GDN_V_HEADS, LANES - 2 * GDN_V_HEADS)).reshape(1, LANES)
    wo = w_out.astype(BF16)
    ng = norm_g.reshape(1, GDN_DV)

    def run(y, b, s0, buf, pad_to):
        t = y.shape[0] // b
        mixed = _norm_matmul(y, g_pre, w_mix).reshape(b, t, conv_dim)
        z = _norm_matmul(y, g_pre, w_z).reshape(b, t, vdim)
        gl = _norm_matmul(y, g_pre, w_g).reshape(b, t, LANES)
        new_buf = jnp.concatenate([buf.astype(F32), mixed], axis=1)[:, t:]
        buf8 = jnp.pad(buf.astype(F32), ((0, 0), (8 - (GDN_CONV - 1), 0), (0, 0)))
        valid_len = None
        if pad_to is not None:
            conv = _gdn_conv(_pad_rows(mixed, 8), buf8, conv_w)
            conv, z, gl = _pad_rows(conv[:, :t], pad_to), _pad_rows(z, pad_to), _pad_rows(gl, pad_to)
            valid_len = t
        else:
            conv = _gdn_conv(mixed, buf8, conv_w)
        beta, gcum, gcum_t = _gdn_gates(gl, a_pad, dt_pad, valid_len)
        core, s_fin = _gdn_scan(conv, z, beta, gcum, gcum_t, ng, s0)
        core = core[:, :t].reshape(b * t, vdim)
        return _proj_res(core, wo, g_post, y), s_fin, new_buf

    tp = yp.shape[0] // bp
    del tp
    s0p = jnp.zeros((bp, GDN_V_HEADS, GDN_DK, GDN_DV), F32)
    buf0 = jnp.zeros((bp, GDN_CONV - 1, conv_dim), F32)
    yp, sp, bufp = run(yp, bp, s0p, buf0, None)
    ys, ss, bufs_new = run(ys, bs, s0s.astype(F32), bufs, GDN_CHUNK)
    return yp, ys, (sp, bufp), (ss, bufs_new)


def kernel(x_prompt, x_sample, cache_k, cache_v, cache_logf, page_table, state_mlstm_C, state_mlstm_n, state_mlstm_m, state_gdn_S, state_gdn_conv, norm_gains, ffn_w_in, ffn_w_down, fox_w_in, fox_b_f, fox_w_out, mlstm_w_in, mlstm_b_gates, mlstm_norm_g, mlstm_w_out, gdn_w_in, gdn_conv_w, gdn_A_log, gdn_dt_bias, gdn_norm_g, gdn_w_out):
    bp, tp, d = x_prompt.shape
    bs, ts, _ = x_sample.shape
    depth = norm_gains.shape[0]
    yp = x_prompt.reshape(bp * tp, d)
    ys = x_sample.reshape(bs * ts, d)

    n_layers, n_pool, page, heads, dh = cache_k.shape
    cache_k2 = cache_k.reshape(n_layers, n_pool, page, heads * dh)
    cache_v2 = cache_v.reshape(n_layers, n_pool, page, heads * dh)
    logf_t = cache_logf.transpose(0, 1, 3, 2)

    fox_p, fox_s, ml_p, ml_s, gd_p, gd_s = [], [], [], [], [], []
    for i in range(depth):
        kind, j = i % 3, i // 3
        g = norm_gains[i].reshape(6, 1, d)
        wg, wu, wd = _ffn_weights(ffn_w_in[i, 0], ffn_w_down[i, 0])
        yp = _half_ffn(yp, g[0], g[1], wg, wu, wd)
        ys = _half_ffn(ys, g[0], g[1], wg, wu, wd)
        if kind == 0:
            yp, ys, po, so = _fox_layer(yp, ys, bp, bs, g[2], g[3], fox_w_in[j], fox_b_f[j], fox_w_out[j],
                                        cache_k2, cache_v2, logf_t, page_table, j)
            fox_p.append(po)
            fox_s.append(so)
        elif kind == 1:
            yp, ys, po, so = _mlstm_layer(yp, ys, bp, bs, g[2], g[3], mlstm_w_in[j], mlstm_b_gates[j],
                                          mlstm_norm_g[j], mlstm_w_out[j],
                                          state_mlstm_C[j], state_mlstm_n[j], state_mlstm_m[j])
            ml_p.append(po)
            ml_s.append(so)
        else:
            yp, ys, po, so = _gdn_layer(yp, ys, bp, bs, g[2], g[3], gdn_w_in[j], gdn_conv_w[j], gdn_A_log[j],
                                        gdn_dt_bias[j], gdn_norm_g[j], gdn_w_out[j],
                                        state_gdn_S[j], state_gdn_conv[j])
            gd_p.append(po)
            gd_s.append(so)
        wg, wu, wd = _ffn_weights(ffn_w_in[i, 1], ffn_w_down[i, 1])
        yp = _half_ffn(yp, g[4], g[5], wg, wu, wd)
        ys = _half_ffn(ys, g[4], g[5], wg, wu, wd)

    stack = lambda items, idx: jnp.stack([it[idx] for it in items])
    return (yp.reshape(bp, tp, d), ys.reshape(bs, ts, d),
            stack(fox_p, 0), stack(fox_p, 1), stack(fox_p, 2),
            stack(fox_s, 0), stack(fox_s, 1), stack(fox_s, 2),
            stack(ml_p, 0), stack(ml_p, 1), stack(ml_p, 2),
            stack(ml_s, 0), stack(ml_s, 1), stack(ml_s, 2),
            stack(gd_p, 0), stack(gd_p, 1),
            stack(gd_s, 0), stack(gd_s, 1))
```
